```python
import jax, jax.numpy as jnp
from jax import lax
import numpy as np

D_MODEL = 2048
BATCH = 4
SEQ = 8192
DEPTH = 4
DEC_BATCH = 16
DEC_SEQ = 16
PAST_LEN = 1024

CHUNK = 64
N_A_LAYERS = DEPTH // 2
N_B_LAYERS = DEPTH - N_A_LAYERS
GLA_HEADS = 4
GLA_DK = D_MODEL // 2
GLA_DV = D_MODEL
GLA_DK_HEAD = GLA_DK // GLA_HEADS
GLA_DV_HEAD = GLA_DV // GLA_HEADS
GLA_GATE_RANK = 16
GLA_GATE_TAU = 16.0
SB_HEADS = 16
SB_HEAD_DIM = D_MODEL // SB_HEADS
SB_WIDTH = SB_HEADS * SB_HEAD_DIM
SB_Q_BLOCK = 128
D_FF = ((8 * D_MODEL // 3 + 255) // 256) * 256
EPS = 1e-6

kernel_name = "yoco_gla_stickbreaking_stream_step"


def rms_norm(x, g):
    xf = x.astype(jnp.float32)
    y = xf * lax.rsqrt(jnp.mean(xf * xf, axis=-1, keepdims=True) + EPS)
    return (y * g.astype(jnp.float32)).astype(x.dtype)


def swiglu_ffn(h, w_in, w_out):
    g, u = jnp.split(h @ w_in, 2, axis=-1)
    return (jax.nn.silu(g) * u) @ w_out


def gla_mixer(h, state0, w_in, w_gate, b_gate, out_norm, w_out):
    B, T, _ = h.shape
    f32 = jnp.float32
    proj = h @ w_in
    q, k, v, r, g_low = jnp.split(
        proj, [GLA_DK, 2 * GLA_DK, 2 * GLA_DK + GLA_DV, 2 * GLA_DK + 2 * GLA_DV], axis=-1)
    log_a = jax.nn.log_sigmoid((g_low @ w_gate + b_gate).astype(f32)) / GLA_GATE_TAU
    L = min(CHUNK, T)
    n_c = T // L

    def heads(t, dh):
        return t.reshape(B, n_c, L, GLA_HEADS, dh).transpose(1, 0, 3, 2, 4)

    qf = heads(q.astype(f32) * (GLA_DK_HEAD ** -0.5), GLA_DK_HEAD)
    kf = heads(k.astype(f32), GLA_DK_HEAD)
    vf = heads(v.astype(f32), GLA_DV_HEAD)
    b = jnp.cumsum(heads(log_a, GLA_DK_HEAD), axis=3)
    b_last = b[:, :, :, -1:, :]
    q_dec = qf * jnp.exp(b)
    k_dec = kf * jnp.exp(b_last - b)
    k_inv = kf * jnp.exp(-b)
    causal = jnp.tril(jnp.ones((L, L), dtype=bool))
    scores = jnp.where(causal, jnp.einsum('cbhtk,cbhsk->cbhts', q_dec, k_inv), 0.0)
    o_intra = jnp.einsum('cbhts,cbhsv->cbhtv', scores, vf)

    def step(S, inp):
        qd, kd, vc, decay = inp
        o = jnp.einsum('bhtk,bhkv->bhtv', qd, S)
        S = decay[..., None] * S + jnp.einsum('bhsk,bhsv->bhkv', kd, vc)
        return S, o

    S_final, o_inter = lax.scan(step, state0.astype(f32),
                                (q_dec, k_dec, vf, jnp.exp(b_last[:, :, :, 0, :])))
    o = (o_intra + o_inter).transpose(1, 0, 3, 2, 4).reshape(B, T, GLA_HEADS, GLA_DV_HEAD)
    o = rms_norm(o, out_norm).reshape(B, T, GLA_DV).astype(h.dtype) * jax.nn.silu(r)
    return o @ w_out, S_final.astype(state0.dtype)


def stick_breaking_attention(q, k, v, past_len):
    B, T, H, Dh = q.shape
    qb = min(SB_Q_BLOCK, T)
    outs = []
    for start in range(0, T, qb):
        end = min(start + qb, T)
        n_keys = past_len + end
        z = jnp.einsum('bthd,bshd->bhts', q[:, start:end], k[:, :n_keys],
                       preferred_element_type=jnp.float32) * (Dh ** -0.5)
        t_pos = past_len + jnp.arange(start, end)
        s_pos = jnp.arange(n_keys)
        mask = s_pos[None, :] < t_pos[:, None]
        log_keep = jnp.where(mask, jax.nn.log_sigmoid(-z), 0.0)
        later = lax.cumsum(log_keep, axis=3, reverse=True) - log_keep
        a = jnp.where(mask, jnp.exp(jax.nn.log_sigmoid(z) + later), 0.0)
        outs.append(jnp.einsum('bhts,bshd->bthd', a.astype(v.dtype), v[:, :n_keys]))
    return jnp.concatenate(outs, axis=1)


def shared_kv(x, kv_norm, w_kv, k_norm):
    B, T, _ = x.shape
    k, v = jnp.split(rms_norm(x, kv_norm) @ w_kv, 2, axis=-1)
    k = rms_norm(k.reshape(B, T, SB_HEADS, SB_HEAD_DIM), k_norm)
    return k, v.reshape(B, T, SB_HEADS, SB_HEAD_DIM)


def trunk(x, past_k, past_v, gla_state, mix_norm, ffn_norm, a_w_in, a_w_gate, a_b_gate,
          a_out_norm, a_w_out, kv_norm, w_kv, k_norm, b_w_q, b_q_norm, b_w_o,
          ffn_w_in, ffn_w_out):
    B, T, _ = x.shape
    past_len = 0 if past_k is None else past_k.shape[1]
    new_gla = []
    k_new = v_new = k_all = v_all = None
    for layer in range(DEPTH):
        h = rms_norm(x, mix_norm[layer])
        if layer < N_A_LAYERS:
            y, s = gla_mixer(h, gla_state[layer], a_w_in[layer], a_w_gate[layer],
                             a_b_gate[layer], a_out_norm[layer], a_w_out[layer])
            new_gla.append(s)
        else:
            j = layer - N_A_LAYERS
            q = rms_norm((h @ b_w_q[j]).reshape(B, T, SB_HEADS, SB_HEAD_DIM), b_q_norm[j])
            o = stick_breaking_attention(q, k_all, v_all, past_len)
            y = o.reshape(B, T, SB_WIDTH) @ b_w_o[j]
        x = x + y
        x = x + swiglu_ffn(rms_norm(x, ffn_norm[layer]), ffn_w_in[layer], ffn_w_out[layer])
        if layer == N_A_LAYERS - 1:
            k_new, v_new = shared_kv(x, kv_norm, w_kv, k_norm)
            if past_k is None:
                k_all, v_all = k_new, v_new
            else:
                k_all = jnp.concatenate([past_k.astype(k_new.dtype), k_new], axis=1)
                v_all = jnp.concatenate([past_v.astype(v_new.dtype), v_new], axis=1)
    return x, k_new, v_new, jnp.stack(new_gla)


def setup_inputs(seed: int = 0) -> dict:
    key = jax.random.key(seed)
    ks = jax.random.split(key, 20)
    f32 = jnp.float32

    def nrm(k, shape, scale):
        return jax.random.normal(k, shape, f32) * scale

    def gain(k, shape):
        return 1.0 + 0.02 * jax.random.normal(k, shape, f32)

    a_in_cols = 2 * GLA_DK + 2 * GLA_DV + GLA_GATE_RANK
    return {
        "x_prompt": nrm(ks[0], (BATCH, SEQ, D_MODEL), 1.0),
        "x_sample": nrm(ks[1], (DEC_BATCH, DEC_SEQ, D_MODEL), 1.0),
        "cache_k": nrm(ks[2], (DEC_BATCH, PAST_LEN, SB_HEADS, SB_HEAD_DIM), 1.0),
        "cache_v": nrm(ks[3], (DEC_BATCH, PAST_LEN, SB_HEADS, SB_HEAD_DIM), 1.0),
        "state_gla": nrm(ks[4], (N_A_LAYERS, DEC_BATCH, GLA_HEADS, GLA_DK_HEAD, GLA_DV_HEAD), 0.5),
        "mix_norm": gain(ks[5], (DEPTH, D_MODEL)),
        "ffn_norm": gain(ks[6], (DEPTH, D_MODEL)),
        "a_w_in": nrm(ks[7], (N_A_LAYERS, D_MODEL, a_in_cols), D_MODEL ** -0.5),
        "a_w_gate": nrm(ks[8], (N_A_LAYERS, GLA_GATE_RANK, GLA_DK), GLA_GATE_RANK ** -0.5),
        "a_b_gate": nrm(ks[9], (N_A_LAYERS, GLA_DK), 0.1),
        "a_out_norm": gain(ks[10], (N_A_LAYERS, GLA_DV_HEAD)),
        "a_w_out": nrm(ks[11], (N_A_LAYERS, GLA_DV, D_MODEL), GLA_DV ** -0.5),
        "kv_norm": gain(ks[12], (D_MODEL,)),
        "w_kv": nrm(ks[13], (D_MODEL, 2 * SB_WIDTH), D_MODEL ** -0.5),
        "k_norm": gain(ks[14], (SB_HEAD_DIM,)),
        "b_w_q": nrm(ks[15], (N_B_LAYERS, D_MODEL, SB_WIDTH), D_MODEL ** -0.5),
        "b_q_norm": gain(ks[16], (N_B_LAYERS, SB_HEAD_DIM)),
        "b_w_o": nrm(ks[17], (N_B_LAYERS, SB_WIDTH, D_MODEL), SB_WIDTH ** -0.5),
        "ffn_w_in": nrm(ks[18], (DEPTH, D_MODEL, 2 * D_FF), D_MODEL ** -0.5),
        "ffn_w_out": nrm(ks[19], (DEPTH, D_FF, D_MODEL), D_FF ** -0.5),
    }


def reference(x_prompt, x_sample, cache_k, cache_v, state_gla, mix_norm, ffn_norm,
              a_w_in, a_w_gate, a_b_gate, a_out_norm, a_w_out, kv_norm, w_kv, k_norm,
              b_w_q, b_q_norm, b_w_o, ffn_w_in, ffn_w_out):
    gla_zero = jnp.zeros((N_A_LAYERS, x_prompt.shape[0], GLA_HEADS, GLA_DK_HEAD, GLA_DV_HEAD),
                         x_prompt.dtype)
    y_prompt, k_prompt, v_prompt, state_gla_prompt = trunk(
        x_prompt, None, None, gla_zero, mix_norm, ffn_norm, a_w_in, a_w_gate, a_b_gate,
        a_out_norm, a_w_out, kv_norm, w_kv, k_norm, b_w_q, b_q_norm, b_w_o,
        ffn_w_in, ffn_w_out)
    y_sample, k_sample, v_sample, state_gla_sample = trunk(
        x_sample, cache_k, cache_v, state_gla, mix_norm, ffn_norm, a_w_in, a_w_gate, a_b_gate,
        a_out_norm, a_w_out, kv_norm, w_kv, k_norm, b_w_q, b_q_norm, b_w_o,
        ffn_w_in, ffn_w_out)
    return (y_prompt, y_sample, k_prompt, v_prompt, state_gla_prompt,
            k_sample, v_sample, state_gla_sample)
```

```python
import functools

import jax
import jax.numpy as jnp
from jax import lax
from jax.experimental import pallas as pl
from jax.experimental.pallas import tpu as pltpu

F32 = jnp.float32
BF16 = jnp.bfloat16

EPS = 1e-6
GLA_CHUNK = 64
GLA_GATE_TAU = 16.0
SB_Q_BLOCK = 128

V7X_LANES = 128
V7X_VMEM_BYTES = 64 * 1024 * 1024
VMEM_LIMIT_BYTES = V7X_VMEM_BYTES * 7 // 8


def _pick_tile(n, target, quantum=V7X_LANES):
    best = None
    t = quantum
    while t <= min(n, target):
        if n % t == 0:
            best = t
        t += quantum
    return n if best is None else best


def _params(semantics):
    return pltpu.CompilerParams(dimension_semantics=semantics,
                                vmem_limit_bytes=VMEM_LIMIT_BYTES)


def _rms(x, gain):
    ms = jnp.mean(x * x, axis=-1, keepdims=True)
    return x * lax.rsqrt(ms + EPS) * gain


def _softplus(z):
    return jnp.maximum(z, 0.0) + jnp.log1p(jnp.exp(-jnp.abs(z)))


def _split_bf16(x):
    hi = x.astype(BF16)
    lo = (x - hi.astype(F32)).astype(BF16)
    return hi, lo


def _norm_matmul_kernel(x_ref, g_ref, w_ref, hg_ref, o_ref, h_ref, *,
                        head_norm_tiles, n_tiles, head_dim, out_scale):
    j = pl.program_id(1)

    @pl.when(j == 0)
    def _():
        h_ref[...] = _rms(x_ref[...], g_ref[...]).astype(BF16)

    acc = jnp.dot(h_ref[...], w_ref[...], preferred_element_type=F32)

    def write_plain():
        o_ref[...] = (acc * out_scale if out_scale != 1.0 else acc).astype(o_ref.dtype)

    def write_head_normed():
        for c in range(acc.shape[1] // head_dim):
            sl = slice(c * head_dim, (c + 1) * head_dim)
            y = _rms(acc[:, sl], hg_ref[...])
            o_ref[:, sl] = (y * out_scale if out_scale != 1.0 else y).astype(o_ref.dtype)

    if head_norm_tiles == 0:
        write_plain()
    elif head_norm_tiles == n_tiles:
        write_head_normed()
    else:
        pl.when(j < head_norm_tiles)(write_head_normed)
        pl.when(j >= head_norm_tiles)(write_plain)


def _norm_matmul(x, gain, w, *, head_gain=None, head_norm_cols=0, out_scale=1.0,
                 out_dtype=F32, tm_target=512, tn_target=1024, name="norm_matmul"):
    n, d = x.shape
    f = w.shape[1]
    tm = _pick_tile(n, tm_target, 8)
    tn = _pick_tile(f, tn_target)
    head_dim = V7X_LANES if head_gain is None else head_gain.shape[-1]
    if head_gain is None:
        head_gain = jnp.ones((head_dim,), F32)
    assert head_norm_cols % tn == 0 and tn % head_dim == 0
    kern = functools.partial(
        _norm_matmul_kernel, head_norm_tiles=head_norm_cols // tn, n_tiles=f // tn,
        head_dim=head_dim, out_scale=out_scale)
    return pl.pallas_call(
        kern,
        grid=(n // tm, f // tn),
        in_specs=[
            pl.BlockSpec((tm, d), lambda i, j: (i, 0)),
            pl.BlockSpec((1, d), lambda i, j: (0, 0)),
            pl.BlockSpec((d, tn), lambda i, j: (0, j)),
            pl.BlockSpec((1, head_dim), lambda i, j: (0, 0)),
        ],
        out_specs=pl.BlockSpec((tm, tn), lambda i, j: (i, j)),
        out_shape=jax.ShapeDtypeStruct((n, f), out_dtype),
        scratch_shapes=[pltpu.VMEM((tm, d), BF16)],
        compiler_params=_params(("parallel", "arbitrary")),
        name=name,
    )(x, gain.reshape(1, d), w, head_gain.reshape(1, head_dim))


def _matmul_residual_kernel(a_ref, w_ref, x_ref, o_ref):
    o_ref[...] = x_ref[...] + jnp.dot(a_ref[...], w_ref[...], preferred_element_type=F32)


def _matmul_residual(a, w, x, *, tm_target=512, tn_target=1024, name="matmul_residual"):
    n, k = a.shape
    f = w.shape[1]
    tm = _pick_tile(n, tm_target, 8)
    tn = _pick_tile(f, tn_target)
    return pl.pallas_call(
        _matmul_residual_kernel,
        grid=(n // tm, f // tn),
        in_specs=[
            pl.BlockSpec((tm, k), lambda i, j: (i, 0)),
            pl.BlockSpec((k, tn), lambda i, j: (0, j)),
            pl.BlockSpec((tm, tn), lambda i, j: (i, j)),
        ],
        out_specs=pl.BlockSpec((tm, tn), lambda i, j: (i, j)),
        out_shape=jax.ShapeDtypeStruct((n, f), F32),
        compiler_params=_params(("parallel", "arbitrary")),
        name=name,
    )(a, w, x)


def _ffn_kernel(x_ref, g_ref, wg_ref, wu_ref, wo_ref, o_ref, h_ref):
    j = pl.program_id(1)

    @pl.when(j == 0)
    def _():
        h_ref[...] = _rms(x_ref[...], g_ref[...]).astype(BF16)

    h = h_ref[...]
    gate = jnp.dot(h, wg_ref[...], preferred_element_type=F32)
    up = jnp.dot(h, wu_ref[...], preferred_element_type=F32)
    act = (gate * jax.nn.sigmoid(gate) * up).astype(BF16)
    part = jnp.dot(act, wo_ref[...], preferred_element_type=F32)

    @pl.when(j == 0)
    def _():
        o_ref[...] = x_ref[...] + part

    @pl.when(j > 0)
    def _():
        o_ref[...] += part


def _ffn(x, gain, w_in, w_out, *, tm_target=512, tf_target=512, name="ffn"):
    n, d = x.shape
    d_ff = w_out.shape[0]
    tm = _pick_tile(n, tm_target, 8)
    tf = _pick_tile(d_ff, tf_target)
    n_f = d_ff // tf
    return pl.pallas_call(
        _ffn_kernel,
        grid=(n // tm, n_f),
        in_specs=[
            pl.BlockSpec((tm, d), lambda i, j: (i, 0)),
            pl.BlockSpec((1, d), lambda i, j: (0, 0)),
            pl.BlockSpec((d, tf), lambda i, j: (0, j)),
            pl.BlockSpec((d, tf), lambda i, j: (0, j + n_f)),
            pl.BlockSpec((tf, d), lambda i, j: (j, 0)),
        ],
        out_specs=pl.BlockSpec((tm, d), lambda i, j: (i, 0)),
        out_shape=jax.ShapeDtypeStruct((n, d), F32),
        scratch_shapes=[pltpu.VMEM((tm, d), BF16)],
        compiler_params=_params(("parallel", "arbitrary")),
        name=name,
    )(x, gain.reshape(1, d), w_in, w_in, w_out)


def _gla_kernel(q_ref, k_ref, v_ref, r_ref, gl_ref, wg_ref, bg_ref, on_ref, s0_ref,
                o_ref, sout_ref, s_ref, *, chunk, n_chunks, q_scale):
    t = pl.program_id(2)
    dk = q_ref.shape[-1]
    dv = v_ref.shape[-1]

    @pl.when(t == 0)
    def _():
        s_ref[...] = s0_ref[...]

    row = lax.broadcasted_iota(jnp.int32, (chunk, chunk), 0)
    col = lax.broadcasted_iota(jnp.int32, (chunk, chunk), 1)
    causal = row >= col
    tri = jnp.where(causal, 1.0, 0.0).astype(BF16)

    def step(c, _):
        sl = pl.ds(pl.multiple_of(c * chunk, chunk), chunk)
        z = jnp.dot(gl_ref[sl, :].astype(BF16), wg_ref[...],
                    preferred_element_type=F32) + bg_ref[...]
        log_a = -_softplus(-z) * (1.0 / GLA_GATE_TAU)
        la_hi, la_lo = _split_bf16(log_a)
        b = (jnp.dot(tri, la_hi, preferred_element_type=F32)
             + jnp.dot(tri, la_lo, preferred_element_type=F32))
        b_last = b[chunk - 1:chunk, :]
        q = q_ref[sl, :] * q_scale
        k = k_ref[sl, :]
        vb = v_ref[sl, :].astype(BF16)
        q_dec = (q * jnp.exp(b)).astype(BF16)
        k_inv = (k * jnp.exp(-b)).astype(BF16)
        k_dec = (k * jnp.exp(b_last - b)).astype(BF16)
        scores = lax.dot_general(q_dec, k_inv, (((1,), (1,)), ((), ())),
                                 preferred_element_type=F32)
        scores = jnp.where(causal, scores, 0.0).astype(BF16)
        state = s_ref[...]
        o = (jnp.dot(scores, vb, preferred_element_type=F32)
             + jnp.dot(q_dec, state.astype(BF16), preferred_element_type=F32))
        kv = lax.dot_general(k_dec, vb, (((0,), (0,)), ((), ())),
                             preferred_element_type=F32)
        decay = jnp.transpose(jnp.broadcast_to(jnp.exp(b_last), (V7X_LANES, dk)))
        for i in range(dv // V7X_LANES):
            ls = slice(i * V7X_LANES, (i + 1) * V7X_LANES)
            s_ref[:, ls] = state[:, ls] * decay + kv[:, ls]
        r = r_ref[sl, :]
        o_ref[sl, :] = (_rms(o, on_ref[...]) * (r * jax.nn.sigmoid(r))).astype(o_ref.dtype)
        return 0

    lax.fori_loop(0, n_chunks, step, 0)

    @pl.when(t == pl.num_programs(2) - 1)
    def _():
        sout_ref[...] = s_ref[...]


def _gla(proj, w_gate, b_gate, out_norm, state0, *, heads, dk, dv, block_target=512):
    bsz, t_len, _ = proj.shape
    dkh, dvh = dk // heads, dv // heads
    chunk = min(GLA_CHUNK, t_len)
    tc = _pick_tile(t_len, block_target, chunk)
    rank_pad = w_gate.shape[0]
    kern = functools.partial(_gla_kernel, chunk=chunk, n_chunks=tc // chunk,
                             q_scale=float(dkh) ** -0.5)
    k_blk0 = dk // dkh
    v_blk0 = 2 * dk // dvh
    r_blk0 = (2 * dk + dv) // dvh
    g_blk0 = (2 * dk + 2 * dv) // rank_pad
    return pl.pallas_call(
        kern,
        grid=(bsz, heads, t_len // tc),
        in_specs=[
            pl.BlockSpec((None, tc, dkh), lambda b, h, t: (b, t, h)),
            pl.BlockSpec((None, tc, dkh), lambda b, h, t: (b, t, k_blk0 + h)),
            pl.BlockSpec((None, tc, dvh), lambda b, h, t: (b, t, v_blk0 + h)),
            pl.BlockSpec((None, tc, dvh), lambda b, h, t: (b, t, r_blk0 + h)),
            pl.BlockSpec((None, tc, rank_pad), lambda b, h, t: (b, t, g_blk0)),
            pl.BlockSpec((rank_pad, dkh), lambda b, h, t: (0, h)),
            pl.BlockSpec((1, dkh), lambda b, h, t: (0, h)),
            pl.BlockSpec((1, dvh), lambda b, h, t: (0, 0)),
            pl.BlockSpec((None, None, dkh, dvh), lambda b, h, t: (b, h, 0, 0)),
        ],
        out_specs=[
            pl.BlockSpec((None, tc, dvh), lambda b, h, t: (b, t, h)),
            pl.BlockSpec((None, None, dkh, dvh), lambda b, h, t: (b, h, 0, 0)),
        ],
        out_shape=[
            jax.ShapeDtypeStruct((bsz, t_len, dv), BF16),
            jax.ShapeDtypeStruct((bsz, heads, dkh, dvh), F32),
        ],
        scratch_shapes=[pltpu.VMEM((dkh, dvh), F32)],
        compiler_params=_params(("parallel", "parallel", "arbitrary")),
        name="gla",
    )(proj, proj, proj, proj, proj, w_gate, b_gate.reshape(1, dk),
      out_norm.reshape(1, dvh), state0)


def _sb_kernel(q_ref, k_ref, v_ref, o_ref, *, bq, bk, past):
    qi = pl.program_id(2)
    q = q_ref[...]
    dh = q.shape[-1]
    q_pos0 = past + qi * bq
    kb_diag = q_pos0 // bk

    jj = lax.broadcasted_iota(jnp.int32, (bk, 2 * bk), 0)
    ss = lax.broadcasted_iota(jnp.int32, (bk, 2 * bk), 1)
    later_and_total = jnp.where((ss >= bk) | (jj > ss), 1.0, 0.0).astype(BF16)

    def block(kb, carry, acc, mask):
        ks = pl.ds(pl.multiple_of(kb * bk, bk), bk)
        z = lax.dot_general(q, k_ref[ks, :], (((1,), (1,)), ((), ())),
                            preferred_element_type=F32)
        sp = _softplus(z)
        log_keep = -sp if mask is None else jnp.where(mask, -sp, 0.0)
        lk_hi, lk_lo = _split_bf16(log_keep)
        sums = (jnp.dot(lk_hi, later_and_total, preferred_element_type=F32)
                + jnp.dot(lk_lo, later_and_total, preferred_element_type=F32))
        a = jnp.exp((z - sp) + sums[:, :bk] + carry)
        if mask is not None:
            a = jnp.where(mask, a, 0.0)
        acc = acc + jnp.dot(a.astype(BF16), v_ref[ks, :], preferred_element_type=F32)
        return carry + sums[:, bk:], acc

    t_pos = q_pos0 + lax.broadcasted_iota(jnp.int32, (bq, bk), 0)
    s_pos = kb_diag * bk + lax.broadcasted_iota(jnp.int32, (bq, bk), 1)
    carry, acc = block(kb_diag, jnp.zeros((bq, bk), F32), jnp.zeros((bq, dh), F32),
                       s_pos < t_pos)

    def body(i, state):
        return block(kb_diag - 1 - i, state[0], state[1], None)

    carry, acc = lax.fori_loop(0, kb_diag, body, (carry, acc))
    o_ref[...] = acc.astype(o_ref.dtype)


def _sb_attention(q, k, v, *, heads, past):
    bsz, t_len, width = q.shape
    dh = width // heads
    tk = k.shape[1]
    bq = min(SB_Q_BLOCK, t_len)
    bk = SB_Q_BLOCK
    assert t_len % bq == 0 and tk % bk == 0 and past % bk == 0 and bq <= bk
    assert tk >= past + t_len
    kern = functools.partial(_sb_kernel, bq=bq, bk=bk, past=past)
    return pl.pallas_call(
        kern,
        grid=(bsz, heads, t_len // bq),
        in_specs=[
            pl.BlockSpec((None, bq, dh), lambda b, h, i: (b, i, h)),
            pl.BlockSpec((None, tk, dh), lambda b, h, i: (b, 0, h)),
            pl.BlockSpec((None, tk, dh), lambda b, h, i: (b, 0, h)),
        ],
        out_specs=pl.BlockSpec((None, bq, dh), lambda b, h, i: (b, i, h)),
        out_shape=jax.ShapeDtypeStruct((bsz, t_len, width), BF16),
        compiler_params=_params(("parallel", "parallel", "arbitrary")),
        name="sb_attention",
    )(q, k, v)


def _trunk(x, past_k, past_v, gla_state, wts):
    bsz, t_len, d = x.shape
    n = bsz * t_len
    n_a = wts["a_w_in"].shape[0]
    depth = wts["ffn_w_in"].shape[0]
    heads_gla = gla_state.shape[2]
    dk = wts["a_b_gate"].shape[-1]
    dv = wts["a_w_out"].shape[1]
    heads_sb = wts["heads_sb"]
    sb_width = wts["b_w_q"].shape[-1]
    dh = sb_width // heads_sb
    past = 0 if past_k is None else past_k.shape[1]

    xf = x.reshape(n, d)
    new_states = []
    k_new = v_new = k_all = v_all = None
    for layer in range(depth):
        if layer < n_a:
            proj = _norm_matmul(xf, wts["mix_norm"][layer], wts["a_w_in"][layer],
                                tn_target=1024, name="gla_in_proj")
            o, s_fin = _gla(proj.reshape(bsz, t_len, -1), wts["a_w_gate"][layer],
                            wts["a_b_gate"][layer], wts["a_out_norm"][layer],
                            gla_state[layer], heads=heads_gla, dk=dk, dv=dv)
            new_states.append(s_fin)
            xf = _matmul_residual(o.reshape(n, dv), wts["a_w_out"][layer], xf,
                                  name="gla_out_proj")
        else:
            j = layer - n_a
            q = _norm_matmul(xf, wts["mix_norm"][layer], wts["b_w_q"][j],
                             head_gain=wts["b_q_norm"][j], head_norm_cols=sb_width,
                             out_scale=float(dh) ** -0.5, out_dtype=BF16, name="sb_q_proj")
            o = _sb_attention(q.reshape(bsz, t_len, sb_width), k_all, v_all,
                              heads=heads_sb, past=past)
            xf = _matmul_residual(o.reshape(n, sb_width), wts["b_w_o"][j], xf,
                                  name="sb_out_proj")
        xf = _ffn(xf, wts["ffn_norm"][layer], wts["ffn_w_in"][layer], wts["ffn_w_out"][layer])
        if layer == n_a - 1:
            k_new = _norm_matmul(xf, wts["kv_norm"], wts["w_k"], head_gain=wts["k_norm"],
                                 head_norm_cols=sb_width, name="shared_k_proj")
            v_new = _norm_matmul(xf, wts["kv_norm"], wts["w_v"], name="shared_v_proj")
            k_new = k_new.reshape(bsz, t_len, heads_sb, dh)
            v_new = v_new.reshape(bsz, t_len, heads_sb, dh)
            k_rows = k_new.reshape(bsz, t_len, sb_width).astype(BF16)
            v_rows = v_new.reshape(bsz, t_len, sb_width).astype(BF16)
            if past_k is not None:
                k_rows = jnp.concatenate(
                    [past_k.reshape(bsz, past, sb_width).astype(BF16), k_rows], axis=1)
                v_rows = jnp.concatenate(
                    [past_v.reshape(bsz, past, sb_width).astype(BF16), v_rows], axis=1)
            pad = -k_rows.shape[1] % SB_Q_BLOCK
            if pad:
                k_rows = jnp.pad(k_rows, ((0, 0), (0, pad), (0, 0)))
                v_rows = jnp.pad(v_rows, ((0, 0), (0, pad), (0, 0)))
            k_all, v_all = k_rows, v_rows
    return xf.reshape(bsz, t_len, d), k_new, v_new, jnp.stack(new_states)


def kernel(x_prompt, x_sample, cache_k, cache_v, state_gla, mix_norm, ffn_norm, a_w_in, a_w_gate, a_b_gate, a_out_norm, a_w_out, kv_norm, w_kv, k_norm, b_w_q, b_q_norm, b_w_o, ffn_w_in, ffn_w_out):
    n_a, rank, dk = a_w_gate.shape
    assert (a_w_in.shape[-1] - rank) % V7X_LANES == 0 and rank <= V7X_LANES
    lane_pad = V7X_LANES - rank
    sb_width = b_w_q.shape[-1]
    wts = {
        "mix_norm": mix_norm, "ffn_norm": ffn_norm, "kv_norm": kv_norm, "k_norm": k_norm,
        "a_b_gate": a_b_gate, "a_out_norm": a_out_norm, "b_q_norm": b_q_norm,
        "heads_sb": cache_k.shape[2],
        "a_w_in": jnp.pad(a_w_in, ((0, 0), (0, 0), (0, lane_pad))).astype(BF16),
        "a_w_gate": jnp.pad(a_w_gate, ((0, 0), (0, V7X_LANES - rank), (0, 0))).astype(BF16),
        "a_w_out": a_w_out.astype(BF16),
        "w_k": w_kv[:, :sb_width].astype(BF16), "w_v": w_kv[:, sb_width:].astype(BF16),
        "b_w_q": b_w_q.astype(BF16), "b_w_o": b_w_o.astype(BF16),
        "ffn_w_in": ffn_w_in.astype(BF16), "ffn_w_out": ffn_w_out.astype(BF16),
    }
    heads_gla, dkh, dvh = state_gla.shape[2:]
    gla_zero = jnp.zeros((n_a, x_prompt.shape[0], heads_gla, dkh, dvh), x_prompt.dtype)
    y_p, k_p, v_p, s_p = _trunk(x_prompt, None, None, gla_zero, wts)
    y_s, k_s, v_s, s_s = _trunk(x_sample, cache_k, cache_v, state_gla, wts)
    return (y_p, y_s, k_p, v_p, s_p, k_s, v_s, s_s)
```

```python
import functools

import jax
import jax.numpy as jnp
from jax import lax
from jax.experimental import pallas as pl
from jax.experimental.pallas import tpu as pltpu

F32 = jnp.float32
BF16 = jnp.bfloat16

LOG2_E = 1.4426950408889634
EPS = 1e-6
GLA_CHUNK = 64
GLA_GATE_TAU = 16.0
SB_QUERY_BLOCK = 1024
SB_KEY_BLOCK = 256
SB_GROUP = 4

V7X_LANES = 128
V7X_VMEM_BYTES = 64 * 1024 * 1024
VMEM_LIMIT_BYTES = V7X_VMEM_BYTES * 7 // 8


def _pick_tile(n, target, quantum=V7X_LANES):
    best = None
    t = quantum
    while t <= min(n, target):
        if n % t == 0:
            best = t
        t += quantum
    return n if best is None else best


def _params(semantics):
    return pltpu.CompilerParams(dimension_semantics=semantics,
                                vmem_limit_bytes=VMEM_LIMIT_BYTES)


def _rms(x, gain):
    ms = jnp.mean(x * x, axis=-1, keepdims=True)
    return x * lax.rsqrt(ms + EPS) * gain


def _softplus(z):
    return jnp.maximum(z, 0.0) + jnp.log1p(jnp.exp(-jnp.abs(z)))


def _split_bf16(x):
    hi = x.astype(BF16)
    lo = (x - hi.astype(F32)).astype(BF16)
    return hi, lo


def _norm_matmul_kernel(x_ref, g_ref, w_ref, hg_ref, o_ref, h_ref, *,
                        head_norm_tiles, n_tiles, head_dim, out_scale):
    j = pl.program_id(1)

    @pl.when(j == 0)
    def _():
        h_ref[...] = _rms(x_ref[...], g_ref[...]).astype(BF16)

    acc = jnp.dot(h_ref[...], w_ref[...], preferred_element_type=F32)

    def write_plain():
        o_ref[...] = (acc * out_scale if out_scale != 1.0 else acc).astype(o_ref.dtype)

    def write_head_normed():
        for c in range(acc.shape[1] // head_dim):
            sl = slice(c * head_dim, (c + 1) * head_dim)
            y = _rms(acc[:, sl], hg_ref[...])
            o_ref[:, sl] = (y * out_scale if out_scale != 1.0 else y).astype(o_ref.dtype)

    if head_norm_tiles == 0:
        write_plain()
    elif head_norm_tiles == n_tiles:
        write_head_normed()
    else:
        pl.when(j < head_norm_tiles)(write_head_normed)
        pl.when(j >= head_norm_tiles)(write_plain)


def _norm_matmul(x, gain, w, *, head_gain=None, head_norm_cols=0, out_scale=1.0,
                 out_dtype=F32, tm_target=512, tn_target=1024, name="norm_matmul"):
    n, d = x.shape
    f = w.shape[1]
    tm = _pick_tile(n, tm_target, 8)
    tn = _pick_tile(f, tn_target)
    head_dim = V7X_LANES if head_gain is None else head_gain.shape[-1]
    if head_gain is None:
        head_gain = jnp.ones((head_dim,), F32)
    assert head_norm_cols % tn == 0 and tn % head_dim == 0
    kern = functools.partial(
        _norm_matmul_kernel, head_norm_tiles=head_norm_cols // tn, n_tiles=f // tn,
        head_dim=head_dim, out_scale=out_scale)
    return pl.pallas_call(
        kern,
        grid=(n // tm, f // tn),
        in_specs=[
            pl.BlockSpec((tm, d), lambda i, j: (i, 0)),
            pl.BlockSpec((1, d), lambda i, j: (0, 0)),
            pl.BlockSpec((d, tn), lambda i, j: (0, j)),
            pl.BlockSpec((1, head_dim), lambda i, j: (0, 0)),
        ],
        out_specs=pl.BlockSpec((tm, tn), lambda i, j: (i, j)),
        out_shape=jax.ShapeDtypeStruct((n, f), out_dtype),
        scratch_shapes=[pltpu.VMEM((tm, d), BF16)],
        compiler_params=_params(("parallel", "arbitrary")),
        name=name,
    )(x, gain.reshape(1, d), w, head_gain.reshape(1, head_dim))


def _matmul_residual_kernel(a_ref, w_ref, x_ref, o_ref):
    o_ref[...] = x_ref[...] + jnp.dot(a_ref[...], w_ref[...], preferred_element_type=F32)


def _matmul_residual(a, w, x, *, tm_target=512, tn_target=1024, name="matmul_residual"):
    n, k = a.shape
    f = w.shape[1]
    tm = _pick_tile(n, tm_target, 8)
    tn = _pick_tile(f, tn_target)
    return pl.pallas_call(
        _matmul_residual_kernel,
        grid=(n // tm, f // tn),
        in_specs=[
            pl.BlockSpec((tm, k), lambda i, j: (i, 0)),
            pl.BlockSpec((k, tn), lambda i, j: (0, j)),
            pl.BlockSpec((tm, tn), lambda i, j: (i, j)),
        ],
        out_specs=pl.BlockSpec((tm, tn), lambda i, j: (i, j)),
        out_shape=jax.ShapeDtypeStruct((n, f), F32),
        compiler_params=_params(("parallel", "arbitrary")),
        name=name,
    )(a, w, x)


def _ffn_kernel(x_ref, g_ref, wg_ref, wu_ref, wo_ref, o_ref, h_ref):
    j = pl.program_id(1)

    @pl.when(j == 0)
    def _():
        h_ref[...] = _rms(x_ref[...], g_ref[...]).astype(BF16)

    h = h_ref[...]
    gate = jnp.dot(h, wg_ref[...], preferred_element_type=F32)
    up = jnp.dot(h, wu_ref[...], preferred_element_type=F32)
    act = (gate * jax.nn.sigmoid(gate) * up).astype(BF16)
    part = jnp.dot(act, wo_ref[...], preferred_element_type=F32)

    @pl.when(j == 0)
    def _():
        o_ref[...] = x_ref[...] + part

    @pl.when(j > 0)
    def _():
        o_ref[...] += part


def _ffn(x, gain, w_in, w_out, *, tm_target=512, tf_target=512, name="ffn"):
    n, d = x.shape
    d_ff = w_out.shape[0]
    tm = _pick_tile(n, tm_target, 8)
    tf = _pick_tile(d_ff, tf_target)
    n_f = d_ff // tf
    return pl.pallas_call(
        _ffn_kernel,
        grid=(n // tm, n_f),
        in_specs=[
            pl.BlockSpec((tm, d), lambda i, j: (i, 0)),
            pl.BlockSpec((1, d), lambda i, j: (0, 0)),
            pl.BlockSpec((d, tf), lambda i, j: (0, j)),
            pl.BlockSpec((d, tf), lambda i, j: (0, j + n_f)),
            pl.BlockSpec((tf, d), lambda i, j: (j, 0)),
        ],
        out_specs=pl.BlockSpec((tm, d), lambda i, j: (i, 0)),
        out_shape=jax.ShapeDtypeStruct((n, d), F32),
        scratch_shapes=[pltpu.VMEM((tm, d), BF16)],
        compiler_params=_params(("parallel", "arbitrary")),
        name=name,
    )(x, gain.reshape(1, d), w_in, w_in, w_out)


def _gla_kernel(q_ref, k_ref, v_ref, r_ref, gl_ref, wg_ref, bg_ref, on_ref, s0_ref,
                o_ref, sout_ref, s_ref, *, chunk, n_chunks, q_scale):
    t = pl.program_id(2)
    dk = q_ref.shape[-1]
    dv = v_ref.shape[-1]

    @pl.when(t == 0)
    def _():
        s_ref[...] = s0_ref[...]

    row = lax.broadcasted_iota(jnp.int32, (chunk, chunk), 0)
    col = lax.broadcasted_iota(jnp.int32, (chunk, chunk), 1)
    causal = row >= col
    tri = jnp.where(causal, 1.0, 0.0).astype(BF16)

    def step(c, _):
        sl = pl.ds(pl.multiple_of(c * chunk, chunk), chunk)
        z = jnp.dot(gl_ref[sl, :].astype(BF16), wg_ref[...],
                    preferred_element_type=F32) + bg_ref[...]
        log_a = -_softplus(-z) * (1.0 / GLA_GATE_TAU)
        la_hi, la_lo = _split_bf16(log_a)
        b = (jnp.dot(tri, la_hi, preferred_element_type=F32)
             + jnp.dot(tri, la_lo, preferred_element_type=F32))
        b_last = b[chunk - 1:chunk, :]
        q = q_ref[sl, :] * q_scale
        k = k_ref[sl, :]
        vb = v_ref[sl, :].astype(BF16)
        q_dec = (q * jnp.exp(b)).astype(BF16)
        k_inv = (k * jnp.exp(-b)).astype(BF16)
        k_dec = (k * jnp.exp(b_last - b)).astype(BF16)
        scores = lax.dot_general(q_dec, k_inv, (((1,), (1,)), ((), ())),
                                 preferred_element_type=F32)
        scores = jnp.where(causal, scores, 0.0).astype(BF16)
        state = s_ref[...]
        o = (jnp.dot(scores, vb, preferred_element_type=F32)
             + jnp.dot(q_dec, state.astype(BF16), preferred_element_type=F32))
        kv = lax.dot_general(k_dec, vb, (((0,), (0,)), ((), ())),
                             preferred_element_type=F32)
        decay = jnp.transpose(jnp.broadcast_to(jnp.exp(b_last), (V7X_LANES, dk)))
        for i in range(dv // V7X_LANES):
            ls = slice(i * V7X_LANES, (i + 1) * V7X_LANES)
            s_ref[:, ls] = state[:, ls] * decay + kv[:, ls]
        r = r_ref[sl, :]
        o_ref[sl, :] = (_rms(o, on_ref[...]) * (r * jax.nn.sigmoid(r))).astype(o_ref.dtype)
        return 0

    lax.fori_loop(0, n_chunks, step, 0)

    @pl.when(t == pl.num_programs(2) - 1)
    def _():
        sout_ref[...] = s_ref[...]


def _gla(proj, w_gate, b_gate, out_norm, state0, *, heads, dk, dv, block_target=512):
    bsz, t_len, _ = proj.shape
    dkh, dvh = dk // heads, dv // heads
    chunk = min(GLA_CHUNK, t_len)
    tc = _pick_tile(t_len, block_target, chunk)
    rank_pad = w_gate.shape[0]
    kern = functools.partial(_gla_kernel, chunk=chunk, n_chunks=tc // chunk,
                             q_scale=float(dkh) ** -0.5)
    k_blk0 = dk // dkh
    v_blk0 = 2 * dk // dvh
    r_blk0 = (2 * dk + dv) // dvh
    g_blk0 = (2 * dk + 2 * dv) // rank_pad
    return pl.pallas_call(
        kern,
        grid=(bsz, heads, t_len // tc),
        in_specs=[
            pl.BlockSpec((None, tc, dkh), lambda b, h, t: (b, t, h)),
            pl.BlockSpec((None, tc, dkh), lambda b, h, t: (b, t, k_blk0 + h)),
            pl.BlockSpec((None, tc, dvh), lambda b, h, t: (b, t, v_blk0 + h)),
            pl.BlockSpec((None, tc, dvh), lambda b, h, t: (b, t, r_blk0 + h)),
            pl.BlockSpec((None, tc, rank_pad), lambda b, h, t: (b, t, g_blk0)),
            pl.BlockSpec((rank_pad, dkh), lambda b, h, t: (0, h)),
            pl.BlockSpec((1, dkh), lambda b, h, t: (0, h)),
            pl.BlockSpec((1, dvh), lambda b, h, t: (0, 0)),
            pl.BlockSpec((None, None, dkh, dvh), lambda b, h, t: (b, h, 0, 0)),
        ],
        out_specs=[
            pl.BlockSpec((None, tc, dvh), lambda b, h, t: (b, t, h)),
            pl.BlockSpec((None, None, dkh, dvh), lambda b, h, t: (b, h, 0, 0)),
        ],
        out_shape=[
            jax.ShapeDtypeStruct((bsz, t_len, dv), BF16),
            jax.ShapeDtypeStruct((bsz, heads, dkh, dvh), F32),
        ],
        scratch_shapes=[pltpu.VMEM((dkh, dvh), F32)],
        compiler_params=_params(("parallel", "parallel", "arbitrary")),
        name="gla",
    )(proj, proj, proj, proj, proj, w_gate, b_gate.reshape(1, dk),
      out_norm.reshape(1, dvh), state0)


def _sb_kernel(q_ref, k_ref, v_ref, o_ref, *, bq, bk, group, past):
    qi = pl.program_id(2)
    q = q_ref[...]
    dh = q.shape[-1]
    span = bk * group
    q_pos0 = past + qi * bq
    n_groups = (q_pos0 + bq - 2) // span + 1

    jj = lax.broadcasted_iota(jnp.int32, (bk, bk), 0)
    ss = lax.broadcasted_iota(jnp.int32, (bk, bk), 1)
    minus_later = jnp.where(jj > ss, -1.0, 0.0).astype(BF16)
    t_minus_lane = (q_pos0 + lax.broadcasted_iota(jnp.int32, (bq, bk), 0)
                    - lax.broadcasted_iota(jnp.int32, (bq, bk), 1))

    n_full = q_pos0 // span

    def scores(g):
        return [lax.dot_general(q, k_ref[pl.ds(pl.multiple_of(g * span + u * bk, bk), bk), :],
                                (((1,), (1,)), ((), ())), preferred_element_type=F32)
                for u in range(group)]

    def weights(g, zs, log_keep_later, masked):
        out = [None] * group
        for u in reversed(range(group)):
            z = zs[u]
            sp = jnp.maximum(z, 0.0) + jnp.log(1.0 + jnp.exp2(jnp.abs(z) * -LOG2_E))
            if masked:
                visible = (g * span + u * bk) < t_minus_lane
                sp_vis = jnp.where(visible, sp, 0.0)
            else:
                sp_vis = sp
            in_block = jnp.dot(sp_vis.astype(BF16), minus_later, preferred_element_type=F32)
            a = jnp.exp(((z - sp) + in_block) + log_keep_later)
            if masked:
                a = jnp.where(visible, a, 0.0)
            out[u] = a.astype(BF16)
            log_keep_later = log_keep_later - jnp.sum(sp_vis, axis=-1, keepdims=True)
        return jnp.concatenate(out, axis=1), log_keep_later

    def step(first_group, masked, i, state):
        log_keep_later, acc = state
        g = first_group - i
        a, log_keep_later = weights(g, scores(g), log_keep_later, masked)
        rows = pl.ds(pl.multiple_of(g * span, span), span)
        return log_keep_later, acc + jnp.dot(a, v_ref[rows, :], preferred_element_type=F32)

    state = (jnp.zeros((bq, 1), F32), jnp.zeros((bq, dh), F32))
    state = lax.fori_loop(0, n_groups - n_full, functools.partial(step, n_groups - 1, True),
                          state)
    _, acc = lax.fori_loop(0, n_full, functools.partial(step, n_full - 1, False), state)
    o_ref[...] = acc.astype(o_ref.dtype)


def _sb_attention(q, k, v, *, heads, past, bq, bk, group):
    bsz, t_len, width = q.shape
    dh = width // heads
    tk = k.shape[1]
    assert t_len % bq == 0 and tk % (bk * group) == 0 and tk >= past + t_len
    kern = functools.partial(_sb_kernel, bq=bq, bk=bk, group=group, past=past)
    return pl.pallas_call(
        kern,
        grid=(bsz, heads, t_len // bq),
        in_specs=[
            pl.BlockSpec((None, bq, dh), lambda b, h, i: (b, i, h)),
            pl.BlockSpec((None, tk, dh), lambda b, h, i: (b, 0, h)),
            pl.BlockSpec((None, tk, dh), lambda b, h, i: (b, 0, h)),
        ],
        out_specs=pl.BlockSpec((None, bq, dh), lambda b, h, i: (b, i, h)),
        out_shape=jax.ShapeDtypeStruct((bsz, t_len, width), BF16),
        compiler_params=_params(("parallel", "parallel", "arbitrary")),
        name="sb_attention",
    )(q, k, v)


def _trunk(x, past_k, past_v, gla_state, wts):
    bsz, t_len, d = x.shape
    n = bsz * t_len
    n_a = wts["a_w_in"].shape[0]
    depth = wts["ffn_w_in"].shape[0]
    heads_gla = gla_state.shape[2]
    dk = wts["a_b_gate"].shape[-1]
    dv = wts["a_w_out"].shape[1]
    heads_sb = wts["heads_sb"]
    sb_width = wts["b_w_q"].shape[-1]
    dh = sb_width // heads_sb
    past = 0 if past_k is None else past_k.shape[1]
    sb_bq = min(SB_QUERY_BLOCK, t_len)
    sb_blocks = -(-(past + t_len) // SB_KEY_BLOCK)
    sb_group = sb_blocks if sb_blocks <= 2 * SB_GROUP else SB_GROUP

    xf = x.reshape(n, d)
    new_states = []
    k_new = v_new = k_all = v_all = None
    for layer in range(depth):
        if layer < n_a:
            proj = _norm_matmul(xf, wts["mix_norm"][layer], wts["a_w_in"][layer],
                                tn_target=1024, name="gla_in_proj")
            o, s_fin = _gla(proj.reshape(bsz, t_len, -1), wts["a_w_gate"][layer],
                            wts["a_b_gate"][layer], wts["a_out_norm"][layer],
                            gla_state[layer], heads=heads_gla, dk=dk, dv=dv)
            new_states.append(s_fin)
            xf = _matmul_residual(o.reshape(n, dv), wts["a_w_out"][layer], xf,
                                  name="gla_out_proj")
        else:
            j = layer - n_a
            q = _norm_matmul(xf, wts["mix_norm"][layer], wts["b_w_q"][j],
                             head_gain=wts["b_q_norm"][j], head_norm_cols=sb_width,
                             out_scale=float(dh) ** -0.5, out_dtype=BF16, name="sb_q_proj")
            o = _sb_attention(q.reshape(bsz, t_len, sb_width), k_all, v_all,
                              heads=heads_sb, past=past, bq=sb_bq, bk=SB_KEY_BLOCK,
                              group=sb_group)
            xf = _matmul_residual(o.reshape(n, sb_width), wts["b_w_o"][j], xf,
                                  name="sb_out_proj")
        xf = _ffn(xf, wts["ffn_norm"][layer], wts["ffn_w_in"][layer], wts["ffn_w_out"][layer])
        if layer == n_a - 1:
            k_new = _norm_matmul(xf, wts["kv_norm"], wts["w_k"], head_gain=wts["k_norm"],
                                 head_norm_cols=sb_width, name="shared_k_proj")
            v_new = _norm_matmul(xf, wts["kv_norm"], wts["w_v"], name="shared_v_proj")
            k_new = k_new.reshape(bsz, t_len, heads_sb, dh)
            v_new = v_new.reshape(bsz, t_len, heads_sb, dh)
            k_rows = k_new.reshape(bsz, t_len, sb_width).astype(BF16)
            v_rows = v_new.reshape(bsz, t_len, sb_width).astype(BF16)
            if past_k is not None:
                k_rows = jnp.concatenate(
                    [past_k.reshape(bsz, past, sb_width).astype(BF16), k_rows], axis=1)
                v_rows = jnp.concatenate(
                    [past_v.reshape(bsz, past, sb_width).astype(BF16), v_rows], axis=1)
            pad = -k_rows.shape[1] % (SB_KEY_BLOCK * sb_group)
            if pad:
                k_rows = jnp.pad(k_rows, ((0, 0), (0, pad), (0, 0)))
                v_rows = jnp.pad(v_rows, ((0, 0), (0, pad), (0, 0)))
            k_all, v_all = k_rows, v_rows
    return xf.reshape(bsz, t_len, d), k_new, v_new, jnp.stack(new_states)


def kernel(x_prompt, x_sample, cache_k, cache_v, state_gla, mix_norm, ffn_norm, a_w_in, a_w_gate, a_b_gate, a_out_norm, a_w_out, kv_norm, w_kv, k_norm, b_w_q, b_q_norm, b_w_o, ffn_w_in, ffn_w_out):
    n_a, rank, dk = a_w_gate.shape
    assert (a_w_in.shape[-1] - rank) % V7X_LANES == 0 and rank <= V7X_LANES
    lane_pad = V7X_LANES - rank
    sb_width = b_w_q.shape[-1]
    wts = {
        "mix_norm": mix_norm, "ffn_norm": ffn_norm, "kv_norm": kv_norm, "k_norm": k_norm,
        "a_b_gate": a_b_gate, "a_out_norm": a_out_norm, "b_q_norm": b_q_norm,
        "heads_sb": cache_k.shape[2],
        "a_w_in": jnp.pad(a_w_in, ((0, 0), (0, 0), (0, lane_pad))).astype(BF16),
        "a_w_gate": jnp.pad(a_w_gate, ((0, 0), (0, V7X_LANES - rank), (0, 0))).astype(BF16),
        "a_w_out": a_w_out.astype(BF16),
        "w_k": w_kv[:, :sb_width].astype(BF16), "w_v": w_kv[:, sb_width:].astype(BF16),
        "b_w_q": b_w_q.astype(BF16), "b_w_o": b_w_o.astype(BF16),
        "ffn_w_in": ffn_w_in.astype(BF16), "ffn_w_out": ffn_w_out.astype(BF16),
    }
    heads_gla, dkh, dvh = state_gla.shape[2:]
    gla_zero = jnp.zeros((n_a, x_prompt.shape[0], heads_gla, dkh, dvh), x_prompt.dtype)
    y_p, k_p, v_p, s_p = _trunk(x_prompt, None, None, gla_zero, wts)
    y_s, k_s, v_s, s_s = _trunk(x_sample, cache_k, cache_v, state_gla, wts)
    return (y_p, y_s, k_p, v_p, s_p, k_s, v_s, s_s)
```

```python
import functools

import jax
import jax.numpy as jnp
from jax import lax
from jax.experimental import pallas as pl
from jax.experimental.pallas import tpu as pltpu

F32 = jnp.float32
BF16 = jnp.bfloat16

LOG2_E = 1.4426950408889634
EPS = 1e-6
GLA_CHUNK = 64
GLA_GATE_TAU = 16.0
SB_QUERY_BLOCK = 1024
SB_KEY_BLOCK = 256
SB_GROUP = 4

V7X_LANES = 128
V7X_VMEM_BYTES = 64 * 1024 * 1024
VMEM_LIMIT_BYTES = V7X_VMEM_BYTES * 7 // 8


def _pick_tile(n, target, quantum=V7X_LANES):
    best = None
    t = quantum
    while t <= min(n, target):
        if n % t == 0:
            best = t
        t += quantum
    return n if best is None else best


def _params(semantics):
    return pltpu.CompilerParams(dimension_semantics=semantics,
                                vmem_limit_bytes=VMEM_LIMIT_BYTES)


def _rms(x, gain):
    ms = jnp.mean(x * x, axis=-1, keepdims=True)
    return x * lax.rsqrt(ms + EPS) * gain


def _softplus(z):
    return jnp.maximum(z, 0.0) + jnp.log1p(jnp.exp(-jnp.abs(z)))


def _split_bf16(x):
    hi = x.astype(BF16)
    lo = (x - hi.astype(F32)).astype(BF16)
    return hi, lo


def _norm_matmul_kernel(x_ref, g_ref, w_ref, hg_ref, o_ref, h_ref, *,
                        head_norm_tiles, n_tiles, head_dim, out_scale):
    j = pl.program_id(1)

    @pl.when(j == 0)
    def _():
        h_ref[...] = _rms(x_ref[...], g_ref[...]).astype(BF16)

    acc = jnp.dot(h_ref[...], w_ref[...], preferred_element_type=F32)

    def write_plain():
        o_ref[...] = (acc * out_scale if out_scale != 1.0 else acc).astype(o_ref.dtype)

    def write_head_normed():
        for c in range(acc.shape[1] // head_dim):
            sl = slice(c * head_dim, (c + 1) * head_dim)
            y = _rms(acc[:, sl], hg_ref[...])
            o_ref[:, sl] = (y * out_scale if out_scale != 1.0 else y).astype(o_ref.dtype)

    if head_norm_tiles == 0:
        write_plain()
    elif head_norm_tiles == n_tiles:
        write_head_normed()
    else:
        pl.when(j < head_norm_tiles)(write_head_normed)
        pl.when(j >= head_norm_tiles)(write_plain)


def _norm_matmul(x, gain, w, *, head_gain=None, head_norm_cols=0, out_scale=1.0,
                 out_dtype=F32, tm_target=512, tn_target=1024, name="norm_matmul"):
    n, d = x.shape
    f = w.shape[1]
    tm = _pick_tile(n, tm_target, 8)
    tn = _pick_tile(f, tn_target)
    head_dim = V7X_LANES if head_gain is None else head_gain.shape[-1]
    if head_gain is None:
        head_gain = jnp.ones((head_dim,), F32)
    assert head_norm_cols % tn == 0 and tn % head_dim == 0
    kern = functools.partial(
        _norm_matmul_kernel, head_norm_tiles=head_norm_cols // tn, n_tiles=f // tn,
        head_dim=head_dim, out_scale=out_scale)
    return pl.pallas_call(
        kern,
        grid=(n // tm, f // tn),
        in_specs=[
            pl.BlockSpec((tm, d), lambda i, j: (i, 0)),
            pl.BlockSpec((1, d), lambda i, j: (0, 0)),
            pl.BlockSpec((d, tn), lambda i, j: (0, j)),
            pl.BlockSpec((1, head_dim), lambda i, j: (0, 0)),
        ],
        out_specs=pl.BlockSpec((tm, tn), lambda i, j: (i, j)),
        out_shape=jax.ShapeDtypeStruct((n, f), out_dtype),
        scratch_shapes=[pltpu.VMEM((tm, d), BF16)],
        compiler_params=_params(("parallel", "arbitrary")),
        name=name,
    )(x, gain.reshape(1, d), w, head_gain.reshape(1, head_dim))


def _matmul_residual_kernel(a_ref, w_ref, x_ref, o_ref):
    o_ref[...] = x_ref[...] + jnp.dot(a_ref[...], w_ref[...], preferred_element_type=F32)


def _matmul_residual(a, w, x, *, tm_target=512, tn_target=1024, name="matmul_residual"):
    n, k = a.shape
    f = w.shape[1]
    tm = _pick_tile(n, tm_target, 8)
    tn = _pick_tile(f, tn_target)
    return pl.pallas_call(
        _matmul_residual_kernel,
        grid=(n // tm, f // tn),
        in_specs=[
            pl.BlockSpec((tm, k), lambda i, j: (i, 0)),
            pl.BlockSpec((k, tn), lambda i, j: (0, j)),
            pl.BlockSpec((tm, tn), lambda i, j: (i, j)),
        ],
        out_specs=pl.BlockSpec((tm, tn), lambda i, j: (i, j)),
        out_shape=jax.ShapeDtypeStruct((n, f), F32),
        compiler_params=_params(("parallel", "arbitrary")),
        name=name,
    )(a, w, x)


def _ffn_kernel(x_ref, g_ref, wg_ref, wu_ref, wo_ref, o_ref, h_ref, act_ref):
    j = pl.program_id(1)

    @pl.when(j == 0)
    def _():
        h_ref[...] = _rms(x_ref[...], g_ref[...]).astype(BF16)
        o_ref[...] = x_ref[...]
        act_ref[1] = jnp.zeros(act_ref.shape[1:], BF16)

    o_ref[...] += jnp.dot(act_ref[(j + 1) % 2], wo_ref[...], preferred_element_type=F32)
    h = h_ref[...]
    gate = jnp.dot(h, wg_ref[...], preferred_element_type=F32)
    up = jnp.dot(h, wu_ref[...], preferred_element_type=F32)
    act_ref[j % 2] = (gate * jax.nn.sigmoid(gate) * up).astype(BF16)


def _ffn(x, gain, w_in, w_out, *, tm_target=512, tf_target=512, name="ffn"):
    n, d = x.shape
    d_ff = w_out.shape[0]
    tm = _pick_tile(n, tm_target, 8)
    tf = _pick_tile(d_ff, tf_target)
    n_f = d_ff // tf
    return pl.pallas_call(
        _ffn_kernel,
        grid=(n // tm, n_f + 1),
        in_specs=[
            pl.BlockSpec((tm, d), lambda i, j: (i, 0)),
            pl.BlockSpec((1, d), lambda i, j: (0, 0)),
            pl.BlockSpec((d, tf), lambda i, j: (0, jnp.minimum(j, n_f - 1))),
            pl.BlockSpec((d, tf), lambda i, j: (0, jnp.minimum(j, n_f - 1) + n_f)),
            pl.BlockSpec((tf, d), lambda i, j: (jnp.maximum(j - 1, 0), 0)),
        ],
        out_specs=pl.BlockSpec((tm, d), lambda i, j: (i, 0)),
        out_shape=jax.ShapeDtypeStruct((n, d), F32),
        scratch_shapes=[pltpu.VMEM((tm, d), BF16), pltpu.VMEM((2, tm, tf), BF16)],
        compiler_params=_params(("parallel", "arbitrary")),
        name=name,
    )(x, gain.reshape(1, d), w_in, w_in, w_out)


def _gla_kernel(q_ref, k_ref, v_ref, r_ref, gl_ref, wg_ref, bg_ref, on_ref, s0_ref,
                o_ref, sout_ref, s_ref, *, chunk, n_chunks, q_scale):
    t = pl.program_id(2)
    dk = q_ref.shape[-1]
    dv = v_ref.shape[-1]

    @pl.when(t == 0)
    def _():
        s_ref[...] = s0_ref[...]

    row = lax.broadcasted_iota(jnp.int32, (chunk, chunk), 0)
    col = lax.broadcasted_iota(jnp.int32, (chunk, chunk), 1)
    causal = row >= col
    tri = jnp.where(causal, 1.0, 0.0).astype(BF16)

    chunks = [slice(c * chunk, (c + 1) * chunk) for c in range(n_chunks)]
    nt_dims = (((1,), (1,)), ((), ()))
    tn_dims = (((0,), (0,)), ((), ()))

    z = jnp.dot(gl_ref[...].astype(BF16), wg_ref[...], preferred_element_type=F32) + bg_ref[...]
    log_a = -_softplus(-z) * (1.0 / GLA_GATE_TAU)
    la_hi, la_lo = _split_bf16(log_a)
    b_chunks = [jnp.dot(tri, la_hi[c], preferred_element_type=F32)
                + jnp.dot(tri, la_lo[c], preferred_element_type=F32) for c in chunks]
    b = jnp.concatenate(b_chunks, axis=0)
    b_last = [bc[chunk - 1:chunk, :] for bc in b_chunks]
    b_last_rows = jnp.concatenate([jnp.broadcast_to(bl, (chunk, dk)) for bl in b_last], axis=0)

    k = k_ref[...]
    q_dec = (q_ref[...] * q_scale * jnp.exp(b)).astype(BF16)
    k_inv = (k * jnp.exp(-b)).astype(BF16)
    k_dec = (k * jnp.exp(b_last_rows - b)).astype(BF16)
    vb = v_ref[...].astype(BF16)

    scores = [lax.dot_general(q_dec[c], k_inv[c], nt_dims, preferred_element_type=F32)
              for c in chunks]
    o_intra = [jnp.dot(jnp.where(causal, sc, 0.0).astype(BF16), vb[c],
                       preferred_element_type=F32) for sc, c in zip(scores, chunks)]
    kv = [lax.dot_general(k_dec[c], vb[c], tn_dims, preferred_element_type=F32)
          for c in chunks]
    decay = [jnp.transpose(jnp.broadcast_to(jnp.exp(bl), (V7X_LANES, dk))) for bl in b_last]

    state = s_ref[...]
    o_inter = []
    for c, kv_c, decay_c in zip(chunks, kv, decay):
        o_inter.append(jnp.dot(q_dec[c], state.astype(BF16), preferred_element_type=F32))
        state = state * jnp.concatenate([decay_c] * (dv // V7X_LANES), axis=1) + kv_c
    s_ref[...] = state

    o = jnp.concatenate(o_intra, axis=0) + jnp.concatenate(o_inter, axis=0)
    r = r_ref[...]
    o_ref[...] = (_rms(o, on_ref[...]) * (r * jax.nn.sigmoid(r))).astype(o_ref.dtype)

    @pl.when(t == pl.num_programs(2) - 1)
    def _():
        sout_ref[...] = s_ref[...]


def _gla(proj, w_gate, b_gate, out_norm, state0, *, heads, dk, dv, block_target=512):
    bsz, t_len, _ = proj.shape
    dkh, dvh = dk // heads, dv // heads
    chunk = min(GLA_CHUNK, t_len)
    tc = _pick_tile(t_len, block_target, chunk)
    rank_pad = w_gate.shape[0]
    kern = functools.partial(_gla_kernel, chunk=chunk, n_chunks=tc // chunk,
                             q_scale=float(dkh) ** -0.5)
    k_blk0 = dk // dkh
    v_blk0 = 2 * dk // dvh
    r_blk0 = (2 * dk + dv) // dvh
    g_blk0 = (2 * dk + 2 * dv) // rank_pad
    return pl.pallas_call(
        kern,
        grid=(bsz, heads, t_len // tc),
        in_specs=[
            pl.BlockSpec((None, tc, dkh), lambda b, h, t: (b, t, h)),
            pl.BlockSpec((None, tc, dkh), lambda b, h, t: (b, t, k_blk0 + h)),
            pl.BlockSpec((None, tc, dvh), lambda b, h, t: (b, t, v_blk0 + h)),
            pl.BlockSpec((None, tc, dvh), lambda b, h, t: (b, t, r_blk0 + h)),
            pl.BlockSpec((None, tc, rank_pad), lambda b, h, t: (b, t, g_blk0)),
            pl.BlockSpec((rank_pad, dkh), lambda b, h, t: (0, h)),
            pl.BlockSpec((1, dkh), lambda b, h, t: (0, h)),
            pl.BlockSpec((1, dvh), lambda b, h, t: (0, 0)),
            pl.BlockSpec((None, None, dkh, dvh), lambda b, h, t: (b, h, 0, 0)),
        ],
        out_specs=[
            pl.BlockSpec((None, tc, dvh), lambda b, h, t: (b, t, h)),
            pl.BlockSpec((None, None, dkh, dvh), lambda b, h, t: (b, h, 0, 0)),
        ],
        out_shape=[
            jax.ShapeDtypeStruct((bsz, t_len, dv), BF16),
            jax.ShapeDtypeStruct((bsz, heads, dkh, dvh), F32),
        ],
        scratch_shapes=[pltpu.VMEM((dkh, dvh), F32)],
        compiler_params=_params(("parallel", "parallel", "arbitrary")),
        name="gla",
    )(proj, proj, proj, proj, proj, w_gate, b_gate.reshape(1, dk),
      out_norm.reshape(1, dvh), state0)


def _sb_kernel(q_ref, k_ref, v_ref, o_ref, *, bq, bk, group, past):
    qi = pl.program_id(2)
    q = q_ref[...]
    dh = q.shape[-1]
    span = bk * group
    q_pos0 = past + qi * bq
    n_groups = (q_pos0 + bq - 2) // span + 1

    jj = lax.broadcasted_iota(jnp.int32, (bk, bk), 0)
    ss = lax.broadcasted_iota(jnp.int32, (bk, bk), 1)
    minus_own_and_later = jnp.where(jj >= ss, -1.0, 0.0).astype(BF16)
    t_minus_lane = (q_pos0 + lax.broadcasted_iota(jnp.int32, (bq, bk), 0)
                    - lax.broadcasted_iota(jnp.int32, (bq, bk), 1))

    n_full = q_pos0 // span

    block_is_group = bq == span and past % span == 0

    def first_row(u, masked):
        return u * bk if (masked and block_is_group) else 0

    def scores(g, masked):
        return [lax.dot_general(q[first_row(u, masked):],
                                k_ref[pl.ds(pl.multiple_of(g * span + u * bk, bk), bk), :],
                                (((1,), (1,)), ((), ())), preferred_element_type=F32)
                for u in range(group)]

    def weights(g, zs, log_keep_later, masked):
        out = [None] * group
        for u in reversed(range(group)):
            lo = first_row(u, masked)
            z = zs[u]
            sp = jnp.maximum(z, 0.0) + jnp.log(1.0 + jnp.exp2(jnp.abs(z) * -LOG2_E))
            if masked:
                visible = (g * span + u * bk) < t_minus_lane[lo:]
                sp_vis = jnp.where(visible, sp, 0.0)
            else:
                sp_vis = sp
            in_block = jnp.dot(sp_vis.astype(BF16), minus_own_and_later,
                               preferred_element_type=F32)
            a = jnp.exp((z + in_block) + log_keep_later[lo:])
            if masked:
                a = jnp.where(visible, a, 0.0)
            a = a.astype(BF16)
            total = jnp.sum(sp_vis, axis=-1, keepdims=True)
            if lo:
                a = jnp.concatenate([jnp.zeros((lo, bk), BF16), a], axis=0)
                total = jnp.concatenate([jnp.zeros((lo, 1), F32), total], axis=0)
            out[u] = a
            log_keep_later = log_keep_later - total
        return jnp.concatenate(out, axis=1), log_keep_later

    def step(first_group, masked, i, state):
        log_keep_later, acc = state
        g = first_group - i
        a, log_keep_later = weights(g, scores(g, masked), log_keep_later, masked)
        rows = pl.ds(pl.multiple_of(g * span, span), span)
        return log_keep_later, acc + jnp.dot(a, v_ref[rows, :], preferred_element_type=F32)

    state = (jnp.zeros((bq, 1), F32), jnp.zeros((bq, dh), F32))
    state = lax.fori_loop(0, n_groups - n_full, functools.partial(step, n_groups - 1, True),
                          state)
    _, acc = lax.fori_loop(0, n_full, functools.partial(step, n_full - 1, False), state)
    o_ref[...] = acc.astype(o_ref.dtype)


def _sb_attention(q, k, v, *, heads, past, bq, bk, group):
    bsz, t_len, width = q.shape
    dh = width // heads
    tk = k.shape[1]
    assert t_len % bq == 0 and tk % (bk * group) == 0 and tk >= past + t_len
    kern = functools.partial(_sb_kernel, bq=bq, bk=bk, group=group, past=past)
    return pl.pallas_call(
        kern,
        grid=(bsz, heads, t_len // bq),
        in_specs=[
            pl.BlockSpec((None, bq, dh), lambda b, h, i: (b, i, h)),
            pl.BlockSpec((None, tk, dh), lambda b, h, i: (b, 0, h)),
            pl.BlockSpec((None, tk, dh), lambda b, h, i: (b, 0, h)),
        ],
        out_specs=pl.BlockSpec((None, bq, dh), lambda b, h, i: (b, i, h)),
        out_shape=jax.ShapeDtypeStruct((bsz, t_len, width), BF16),
        compiler_params=_params(("parallel", "parallel", "arbitrary")),
        name="sb_attention",
    )(q, k, v)


def _trunk(x, past_k, past_v, gla_state, wts):
    bsz, t_len, d = x.shape
    n = bsz * t_len
    n_a = wts["a_w_in"].shape[0]
    depth = wts["ffn_w_in"].shape[0]
    heads_gla = gla_state.shape[2]
    dk = wts["a_b_gate"].shape[-1]
    dv = wts["a_w_out"].shape[1]
    heads_sb = wts["heads_sb"]
    sb_width = wts["b_w_q"].shape[-1]
    dh = sb_width // heads_sb
    past = 0 if past_k is None else past_k.shape[1]
    sb_bq = min(SB_QUERY_BLOCK, t_len)
    sb_blocks = -(-(past + t_len) // SB_KEY_BLOCK)
    sb_group = sb_blocks if sb_blocks <= 2 * SB_GROUP else SB_GROUP

    xf = x.reshape(n, d)
    new_states = []
    k_new = v_new = k_all = v_all = None
    for layer in range(depth):
        if layer < n_a:
            proj = _norm_matmul(xf, wts["mix_norm"][layer], wts["a_w_in"][layer],
                                tn_target=1024, name="gla_in_proj")
            o, s_fin = _gla(proj.reshape(bsz, t_len, -1), wts["a_w_gate"][layer],
                            wts["a_b_gate"][layer], wts["a_out_norm"][layer],
                            gla_state[layer], heads=heads_gla, dk=dk, dv=dv)
            new_states.append(s_fin)
            xf = _matmul_residual(o.reshape(n, dv), wts["a_w_out"][layer], xf,
                                  name="gla_out_proj")
        else:
            j = layer - n_a
            q = _norm_matmul(xf, wts["mix_norm"][layer], wts["b_w_q"][j],
                             head_gain=wts["b_q_norm"][j], head_norm_cols=sb_width,
                             out_scale=float(dh) ** -0.5, out_dtype=BF16, name="sb_q_proj")
            o = _sb_attention(q.reshape(bsz, t_len, sb_width), k_all, v_all,
                              heads=heads_sb, past=past, bq=sb_bq, bk=SB_KEY_BLOCK,
                              group=sb_group)
            xf = _matmul_residual(o.reshape(n, sb_width), wts["b_w_o"][j], xf,
                                  name="sb_out_proj")
        xf = _ffn(xf, wts["ffn_norm"][layer], wts["ffn_w_in"][layer], wts["ffn_w_out"][layer])
        if layer == n_a - 1:
            k_new = _norm_matmul(xf, wts["kv_norm"], wts["w_k"], head_gain=wts["k_norm"],
                                 head_norm_cols=sb_width, name="shared_k_proj")
            v_new = _norm_matmul(xf, wts["kv_norm"], wts["w_v"], name="shared_v_proj")
            k_new = k_new.reshape(bsz, t_len, heads_sb, dh)
            v_new = v_new.reshape(bsz, t_len, heads_sb, dh)
            k_rows = k_new.reshape(bsz, t_len, sb_width).astype(BF16)
            v_rows = v_new.reshape(bsz, t_len, sb_width).astype(BF16)
            if past_k is not None:
                k_rows = jnp.concatenate(
                    [past_k.reshape(bsz, past, sb_width).astype(BF16), k_rows], axis=1)
                v_rows = jnp.concatenate(
                    [past_v.reshape(bsz, past, sb_width).astype(BF16), v_rows], axis=1)
            pad = -k_rows.shape[1] % (SB_KEY_BLOCK * sb_group)
            if pad:
                k_rows = jnp.pad(k_rows, ((0, 0), (0, pad), (0, 0)))
                v_rows = jnp.pad(v_rows, ((0, 0), (0, pad), (0, 0)))
            k_all, v_all = k_rows, v_rows
    return xf.reshape(bsz, t_len, d), k_new, v_new, jnp.stack(new_states)


def kernel(x_prompt, x_sample, cache_k, cache_v, state_gla, mix_norm, ffn_norm, a_w_in, a_w_gate, a_b_gate, a_out_norm, a_w_out, kv_norm, w_kv, k_norm, b_w_q, b_q_norm, b_w_o, ffn_w_in, ffn_w_out):
    n_a, rank, dk = a_w_gate.shape
    assert (a_w_in.shape[-1] - rank) % V7X_LANES == 0 and rank <= V7X_LANES
    lane_pad = V7X_LANES - rank
    sb_width = b_w_q.shape[-1]
    wts = {
        "mix_norm": mix_norm, "ffn_norm": ffn_norm, "kv_norm": kv_norm, "k_norm": k_norm,
        "a_b_gate": a_b_gate, "a_out_norm": a_out_norm, "b_q_norm": b_q_norm,
        "heads_sb": cache_k.shape[2],
        "a_w_in": jnp.pad(a_w_in, ((0, 0), (0, 0), (0, lane_pad))).astype(BF16),
        "a_w_gate": jnp.pad(a_w_gate, ((0, 0), (0, V7X_LANES - rank), (0, 0))).astype(BF16),
        "a_w_out": a_w_out.astype(BF16),
        "w_k": w_kv[:, :sb_width].astype(BF16), "w_v": w_kv[:, sb_width:].astype(BF16),
        "b_w_q": b_w_q.astype(BF16), "b_w_o": b_w_o.astype(BF16),
        "ffn_w_in": ffn_w_in.astype(BF16), "ffn_w_out": ffn_w_out.astype(BF16),
    }
    heads_gla, dkh, dvh = state_gla.shape[2:]
    gla_zero = jnp.zeros((n_a, x_prompt.shape[0], heads_gla, dkh, dvh), x_prompt.dtype)
    y_p, k_p, v_p, s_p = _trunk(x_prompt, None, None, gla_zero, wts)
    y_s, k_s, v_s, s_s = _trunk(x_sample, cache_k, cache_v, state_gla, wts)
    return (y_p, y_s, k_p, v_p, s_p, k_s, v_s, s_s)
```
